```python
import jax, jax.numpy as jnp
from jax import lax
import numpy as np

D_MODEL = 2048
BATCH = 16
SEQ = 2048
DEPTH = 2

CHUNK = 64
N_MIXERS = 2
EPS = 1e-6

HGRN_HEADS = 16
HGRN_DK = 128
HGRN_DV = D_MODEL // HGRN_HEADS
HGRN_KEY = HGRN_HEADS * HGRN_DK
HGRN_VAL = HGRN_HEADS * HGRN_DV

SGU_BLOCK = 128
SGU_GROUPS = 16
SGU_WIDTH = D_MODEL
SGU_GROUP_DIM = SGU_WIDTH // SGU_GROUPS

D_FF = 5632
N_EXPERTS = 8
TOP_K = 2
D_FF_EXPERT = 7168

kernel_name = "hybrid_hgrn2_gmlp_moe_trunk"


def rms_norm(x, g):
    xf = x.astype(jnp.float32)
    y = xf * lax.rsqrt(jnp.mean(xf * xf, axis=-1, keepdims=True) + EPS)
    return (y * g.astype(jnp.float32)).astype(x.dtype)


def layer_norm(x, g, b):
    xf = x.astype(jnp.float32)
    mu = jnp.mean(xf, axis=-1, keepdims=True)
    xc = xf - mu
    y = xc * lax.rsqrt(jnp.mean(xc * xc, axis=-1, keepdims=True) + EPS)
    return (y * g.astype(jnp.float32) + b.astype(jnp.float32)).astype(x.dtype)


def hgrn2_chunk_scan(q, k, v, g):
    B, T, H, DK = q.shape
    DV = v.shape[-1]
    n = T // CHUNK

    def to_chunks(a):
        return a.reshape(B, n, CHUNK, H, a.shape[-1]).transpose(1, 0, 3, 2, 4)

    q, k, v, g = to_chunks(q), to_chunks(k), to_chunks(v), to_chunks(g)
    b = jnp.cumsum(g, axis=-2)
    b_mid = b[..., CHUNK // 2 - 1:CHUNK // 2, :]
    b_last = b[..., -1:, :]
    q_intra = q * jnp.exp(b - b_mid)
    k_intra = k * jnp.exp(b_mid - b)
    q_inter = q * jnp.exp(b)
    k_state = k * jnp.exp(b_last - b)
    decay = jnp.exp(b_last[..., 0, :])
    causal = jnp.tril(jnp.ones((CHUNK, CHUNK), dtype=bool))

    def step(S, xs):
        qi, ki, qe, ks, vc, dc = xs
        scores = jnp.where(causal, jnp.einsum('bhtk,bhsk->bhts', qi, ki), 0.0)
        o = (jnp.einsum('bhts,bhsv->bhtv', scores, vc)
             + jnp.einsum('bhtk,bhkv->bhtv', qe, S))
        S = dc[..., None] * S + jnp.einsum('bhsk,bhsv->bhkv', ks, vc)
        return S, o

    S0 = jnp.zeros((B, H, DK, DV), jnp.float32)
    _, o = lax.scan(step, S0, (q_intra, k_intra, q_inter, k_state, v, decay))
    return o.transpose(1, 0, 3, 2, 4).reshape(B, T, H, DV)


def hgrn2_mixer(h, w_in, lb, norm_g, w_out):
    B, T, _ = h.shape
    proj = h @ w_in
    q, f, i, gate = jnp.split(proj, [HGRN_KEY, 2 * HGRN_KEY, 2 * HGRN_KEY + HGRN_VAL], axis=-1)
    q = jax.nn.silu(q.astype(jnp.float32))
    f = f.astype(jnp.float32)
    forget = lb + (1.0 - lb) * jax.nn.sigmoid(f)
    log_f = jnp.log(forget)
    k = (1.0 - lb) * jax.nn.sigmoid(-f)
    o = hgrn2_chunk_scan(q.reshape(B, T, HGRN_HEADS, HGRN_DK),
                         k.reshape(B, T, HGRN_HEADS, HGRN_DK),
                         i.astype(jnp.float32).reshape(B, T, HGRN_HEADS, HGRN_DV),
                         log_f.reshape(B, T, HGRN_HEADS, HGRN_DK))
    o = o * lax.rsqrt(jnp.mean(o * o, axis=-1, keepdims=True) + EPS)
    o = o.reshape(B, T, HGRN_VAL) * norm_g.astype(jnp.float32) * jax.nn.silu(gate.astype(jnp.float32))
    return o.astype(h.dtype) @ w_out


def sgu_mixer(h, w_in, ln_g, ln_b, w_s, b_s, w_out):
    B, T, _ = h.shape
    uv = jax.nn.gelu(h @ w_in, approximate=False)
    u, v = jnp.split(uv, 2, axis=-1)
    v = layer_norm(v, ln_g, ln_b)
    n = T // SGU_BLOCK
    v = v.reshape(B, n, SGU_BLOCK, SGU_GROUPS, SGU_GROUP_DIM)
    pos_chunk = jnp.arange(SGU_BLOCK) // CHUNK
    mask = pos_chunk[:, None] >= pos_chunk[None, :]
    ws = jnp.where(mask[None], w_s, 0.0).astype(v.dtype)
    v = jnp.einsum('gts,bnsgc->bntgc', ws, v) + b_s.T[:, :, None].astype(v.dtype)
    return (u * v.reshape(B, T, SGU_WIDTH)) @ w_out


def swiglu(h, w_gate, w_up, w_down):
    return (jax.nn.silu(h @ w_gate) * (h @ w_up)) @ w_down


def moe_swiglu(h, w_router, w_gate, w_up, w_down):
    logits = (h @ w_router).astype(jnp.float32)
    top_val, top_idx = lax.top_k(logits, TOP_K)
    top_w = jax.nn.softmax(top_val, axis=-1)
    gates = jnp.sum(jax.nn.one_hot(top_idx, N_EXPERTS, dtype=jnp.float32) * top_w[..., None], axis=-2)
    gates = gates.astype(h.dtype)
    y = jnp.zeros_like(h)
    for e in range(N_EXPERTS):
        y = y + gates[..., e:e + 1] * swiglu(h, w_gate[e], w_up[e], w_down[e])
    return y


def setup_inputs(seed: int = 0) -> dict:
    key = jax.random.key(seed)
    ks = jax.random.split(key, 24)
    f32 = jnp.float32
    D = D_MODEL

    def nrm(k, shape, scale):
        return jax.random.normal(k, shape, f32) * scale

    def gain(k, n):
        return 1.0 + 0.05 * jax.random.normal(k, (n,), f32)

    return {
        "x": nrm(ks[0], (BATCH, SEQ, D), 1.0),
        "lb_logits": nrm(ks[1], (DEPTH + 1, HGRN_KEY), 0.1),
        "l0_norm_mix": gain(ks[2], D),
        "l0_hgrn_w_in": nrm(ks[3], (D, 2 * HGRN_KEY + 2 * HGRN_VAL), D ** -0.5),
        "l0_hgrn_norm": gain(ks[4], HGRN_VAL),
        "l0_hgrn_w_out": nrm(ks[5], (HGRN_VAL, D), HGRN_VAL ** -0.5),
        "l0_norm_ffn": gain(ks[6], D),
        "l0_ffn_w_gate": nrm(ks[7], (D, D_FF), D ** -0.5),
        "l0_ffn_w_up": nrm(ks[8], (D, D_FF), D ** -0.5),
        "l0_ffn_w_down": nrm(ks[9], (D_FF, D), D_FF ** -0.5),
        "l1_norm_mix": gain(ks[10], D),
        "l1_sgu_w_in": nrm(ks[11], (D, 2 * SGU_WIDTH), D ** -0.5),
        "l1_sgu_ln_g": gain(ks[12], SGU_WIDTH),
        "l1_sgu_ln_b": nrm(ks[13], (SGU_WIDTH,), 0.02),
        "l1_sgu_w_s": nrm(ks[14], (SGU_GROUPS, SGU_BLOCK, SGU_BLOCK), SGU_BLOCK ** -0.5),
        "l1_sgu_b_s": 1.0 + nrm(ks[15], (SGU_GROUPS, SGU_BLOCK), 0.1),
        "l1_sgu_w_out": nrm(ks[16], (SGU_WIDTH, D), SGU_WIDTH ** -0.5),
        "l1_norm_ffn": gain(ks[17], D),
        "l1_moe_w_router": nrm(ks[18], (D, N_EXPERTS), D ** -0.5),
        "l1_moe_w_gate": nrm(ks[19], (N_EXPERTS, D, D_FF_EXPERT), D ** -0.5),
        "l1_moe_w_up": nrm(ks[20], (N_EXPERTS, D, D_FF_EXPERT), D ** -0.5),
        "l1_moe_w_down": nrm(ks[21], (N_EXPERTS, D_FF_EXPERT, D), D_FF_EXPERT ** -0.5),
        "final_norm": gain(ks[22], D),
    }


def reference(x, lb_logits,
              l0_norm_mix, l0_hgrn_w_in, l0_hgrn_norm, l0_hgrn_w_out,
              l0_norm_ffn, l0_ffn_w_gate, l0_ffn_w_up, l0_ffn_w_down,
              l1_norm_mix, l1_sgu_w_in, l1_sgu_ln_g, l1_sgu_ln_b, l1_sgu_w_s, l1_sgu_b_s, l1_sgu_w_out,
              l1_norm_ffn, l1_moe_w_router, l1_moe_w_gate, l1_moe_w_up, l1_moe_w_down,
              final_norm):
    lb_all = jnp.cumsum(jax.nn.softmax(lb_logits.astype(jnp.float32), axis=0), axis=0)

    mixers = [
        lambda h: hgrn2_mixer(h, l0_hgrn_w_in, lb_all[0], l0_hgrn_norm, l0_hgrn_w_out),
        lambda h: sgu_mixer(h, l1_sgu_w_in, l1_sgu_ln_g, l1_sgu_ln_b, l1_sgu_w_s, l1_sgu_b_s, l1_sgu_w_out),
    ]
    ffns = [
        lambda h: swiglu(h, l0_ffn_w_gate, l0_ffn_w_up, l0_ffn_w_down),
        lambda h: moe_swiglu(h, l1_moe_w_router, l1_moe_w_gate, l1_moe_w_up, l1_moe_w_down),
    ]
    norm_mix = [l0_norm_mix, l1_norm_mix]
    norm_ffn = [l0_norm_ffn, l1_norm_ffn]

    for i in range(DEPTH):
        x = x + mixers[i](rms_norm(x, norm_mix[i]))
        x = x + ffns[i](rms_norm(x, norm_ffn[i]))
    return rms_norm(x, final_norm)
```

```python
import functools
import math

import jax
import jax.numpy as jnp
from jax import lax
from jax.experimental import pallas as pl
from jax.experimental.pallas import tpu as pltpu

D_MODEL = 2048
CHUNK = 64
EPS = 1e-6
HGRN_HEADS = 16
HEAD_DIM = 128
SGU_BLOCK = 128
SGU_GROUPS = 16
SGU_GROUP_DIM = D_MODEL // SGU_GROUPS
N_EXPERTS = 8
LANES = 128
V7X_VMEM_LIMIT_BYTES = 56 * 1024 * 1024

BF16 = jnp.bfloat16
F32 = jnp.float32


def _params(*sem):
    return pltpu.CompilerParams(dimension_semantics=sem, vmem_limit_bytes=V7X_VMEM_LIMIT_BYTES)


def _const_spec(shape):
    nd = len(shape)
    return pl.BlockSpec(shape, lambda *_: (0,) * nd, pipeline_mode=pl.Buffered(1))


def _rms(x, g):
    return x * lax.rsqrt(jnp.mean(x * x, axis=-1, keepdims=True) + EPS) * g


def _sigmoid_pair(f):
    z = jnp.exp(-jnp.abs(f))
    r = 1.0 / (1.0 + z)
    zr = z * r
    pos = f >= 0
    return jnp.where(pos, r, zr), jnp.where(pos, zr, r)


def _silu(x):
    return x * _sigmoid_pair(x)[0]


def _dot(a, b):
    return jnp.dot(a, b, preferred_element_type=F32)


def _rms_cast_kernel(x_ref, g_ref, o_ref):
    o_ref[...] = _rms(x_ref[...], g_ref[...]).astype(o_ref.dtype)


def _rms_cast(x2d, g, tm=512):
    t, d = x2d.shape
    return pl.pallas_call(
        _rms_cast_kernel,
        grid=(t // tm,),
        in_specs=[pl.BlockSpec((tm, d), lambda i: (i, 0)), _const_spec((1, d))],
        out_specs=pl.BlockSpec((tm, d), lambda i: (i, 0)),
        out_shape=jax.ShapeDtypeStruct((t, d), BF16),
        compiler_params=_params("parallel"),
        name="rms_cast",
    )(x2d, g.reshape(1, d))


def _chunk_cumsum(g):
    row = lax.broadcasted_iota(jnp.int32, g.shape, 0) & (CHUNK - 1)
    b = g
    s = 1
    while s < CHUNK:
        b = b + jnp.where(row >= s, pltpu.roll(b, s, axis=0), 0.0)
        s *= 2
    return b


def _hgrn_kernel(h_ref, wq_ref, wf_ref, wi_ref, wg_ref, lbl_ref, ng_ref, o_ref,
                 w_scr, proj_scr, s_scr, *, group):
    seq = h_ref.shape[1]
    dk = HEAD_DIM
    for n, w_ref in enumerate((wq_ref, wf_ref, wi_ref, wg_ref)):
        w_scr[:, n * dk:(n + 1) * dk] = w_ref[...].astype(BF16)
    proj_scr[...] = _dot(h_ref[0], w_scr[...])

    lbl = lbl_ref[...]
    lbe = jnp.exp(lbl - jnp.max(lbl, axis=0, keepdims=True))
    lb = lbe[0:1, :] / jnp.sum(lbe, axis=0, keepdims=True)
    ng = ng_ref[...]

    rows_per_group = group * CHUNK
    t_idx = lax.broadcasted_iota(jnp.int32, (group, CHUNK, CHUNK), 1)
    s_idx = lax.broadcasted_iota(jnp.int32, (group, CHUNK, CHUNK), 2)
    causal = t_idx >= s_idx
    st = jnp.zeros((dk, dk), F32)
    for grp in range(seq // rows_per_group):
        rows = pl.ds(grp * rows_per_group, rows_per_group)
        q = _silu(proj_scr[rows, 0 * dk:1 * dk])
        sig, nsig = _sigmoid_pair(proj_scr[rows, 1 * dk:2 * dk])
        v = proj_scr[rows, 2 * dk:3 * dk]
        gate = proj_scr[rows, 3 * dk:4 * dk]
        g = jnp.log(lb + (1.0 - lb) * sig)
        k = (1.0 - lb) * nsig
        b = _chunk_cumsum(g).reshape(group, CHUNK, dk)
        b_mid = b[:, CHUNK // 2 - 1:CHUNK // 2, :]
        b_last = b[:, CHUNK - 1:CHUNK, :]
        q_intra = q.reshape(group, CHUNK, dk) * jnp.exp(b - b_mid)
        k_intra = k.reshape(group, CHUNK, dk) * jnp.exp(b_mid - b)
        q_inter = (q_intra * jnp.exp(b_mid)).astype(BF16)
        k_state = (k_intra * jnp.exp(b_last - b_mid)).astype(BF16)
        decay = jnp.exp(b_last)
        vb = v.reshape(group, CHUNK, dk).astype(BF16)

        scores = jnp.einsum("gtk,gsk->gts", q_intra.astype(BF16), k_intra.astype(BF16),
                            preferred_element_type=F32)
        scores = jnp.where(causal, scores, 0.0).astype(BF16)
        o = jnp.einsum("gts,gsv->gtv", scores, vb, preferred_element_type=F32)
        d_st = jnp.einsum("gsv,gsk->gvk", vb, k_state, preferred_element_type=F32)
        for c in range(group):
            s_scr[c] = st.astype(BF16)
            st = st * decay[c] + d_st[c]
        o = o + jnp.einsum("gtk,gvk->gtv", q_inter, s_scr[...], preferred_element_type=F32)

        o = o.reshape(rows_per_group, dk)
        o = o * lax.rsqrt(jnp.mean(o * o, axis=-1, keepdims=True) + EPS)
        o_ref[0, rows, :] = (o * ng * _silu(gate)).astype(o_ref.dtype)


def _hgrn(h, w_in, lb_logits, norm_g, group=8):
    bsz, seq, d = h.shape
    dk = HEAD_DIM
    nh = HGRN_HEADS
    w_specs = [pl.BlockSpec((d, dk), lambda b, hd, n=n: (0, n * nh + hd)) for n in range(4)]
    return pl.pallas_call(
        functools.partial(_hgrn_kernel, group=group),
        grid=(bsz, nh),
        in_specs=[pl.BlockSpec((1, seq, d), lambda b, hd: (b, 0, 0))] + w_specs + [
            pl.BlockSpec((lb_logits.shape[0], dk), lambda b, hd: (0, hd)),
            pl.BlockSpec((1, dk), lambda b, hd: (0, hd)),
        ],
        out_specs=pl.BlockSpec((1, seq, dk), lambda b, hd: (b, 0, hd)),
        out_shape=jax.ShapeDtypeStruct((bsz, seq, nh * dk), BF16),
        scratch_shapes=[
            pltpu.VMEM((d, 4 * dk), BF16),
            pltpu.VMEM((seq, 4 * dk), F32),
            pltpu.VMEM((group, dk, dk), BF16),
        ],
        compiler_params=_params("parallel", "arbitrary"),
        name="hgrn2",
    )(h, w_in, w_in, w_in, w_in, lb_logits, norm_g.reshape(1, -1))


def _proj_res_kernel(x_ref, a_ref, w_ref, o_ref):
    o_ref[...] = x_ref[...] + _dot(a_ref[...], w_ref[...])


def _proj_res(x2d, a2d, w_bf16, tm=512):
    t, d = x2d.shape
    k = a2d.shape[1]
    return pl.pallas_call(
        _proj_res_kernel,
        grid=(t // tm,),
        in_specs=[pl.BlockSpec((tm, d), lambda i: (i, 0)),
                  pl.BlockSpec((tm, k), lambda i: (i, 0)),
                  _const_spec((k, d))],
        out_specs=pl.BlockSpec((tm, d), lambda i: (i, 0)),
        out_shape=jax.ShapeDtypeStruct((t, d), F32),
        compiler_params=_params("parallel"),
        name="proj_res",
    )(x2d, a2d, w_bf16)


def _ffn_kernel(x_ref, g_ref, wg_ref, wu_ref, wd_ref, o_ref, h_scr):
    @pl.when(pl.program_id(1) == 0)
    def _():
        x = x_ref[...]
        h_scr[...] = _rms(x, g_ref[...]).astype(BF16)
        o_ref[...] = x

    h = h_scr[...]
    a = _dot(h, wg_ref[...])
    u = _dot(h, wu_ref[...])
    o_ref[...] += _dot((_silu(a) * u).astype(BF16), wd_ref[...])


def _ffn(x2d, g, w_gate, w_up, w_down, tm=512, tf=512):
    t, d = x2d.shape
    dff = w_gate.shape[1]
    return pl.pallas_call(
        _ffn_kernel,
        grid=(t // tm, dff // tf),
        in_specs=[pl.BlockSpec((tm, d), lambda i, j: (i, 0)),
                  _const_spec((1, d)),
                  pl.BlockSpec((d, tf), lambda i, j: (0, j)),
                  pl.BlockSpec((d, tf), lambda i, j: (0, j)),
                  pl.BlockSpec((tf, d), lambda i, j: (j, 0))],
        out_specs=pl.BlockSpec((tm, d), lambda i, j: (i, 0)),
        out_shape=jax.ShapeDtypeStruct((t, d), F32),
        scratch_shapes=[pltpu.VMEM((tm, d), BF16)],
        compiler_params=_params("parallel", "arbitrary"),
        name="ffn_dense",
    )(x2d, g.reshape(1, d), w_gate, w_up, w_down)


def _sgu_in_kernel(x_ref, g_ref, w_ref, o_ref, h_scr):
    @pl.when(pl.program_id(1) == 0)
    def _():
        h_scr[...] = _rms(x_ref[...], g_ref[...]).astype(BF16)

    y = _dot(h_scr[...], w_ref[...])
    o_ref[...] = (0.5 * y * (1.0 + lax.erf(y * math.sqrt(0.5)))).astype(o_ref.dtype)


def _sgu_in(x2d, g, w_bf16, tm=512, tn=1024):
    t, d = x2d.shape
    n = w_bf16.shape[1]
    return pl.pallas_call(
        _sgu_in_kernel,
        grid=(t // tm, n // tn),
        in_specs=[pl.BlockSpec((tm, d), lambda i, j: (i, 0)),
                  _const_spec((1, d)),
                  pl.BlockSpec((d, tn), lambda i, j: (0, j))],
        out_specs=pl.BlockSpec((tm, tn), lambda i, j: (i, j)),
        out_shape=jax.ShapeDtypeStruct((t, n), BF16),
        scratch_shapes=[pltpu.VMEM((tm, d), BF16)],
        compiler_params=_params("parallel", "arbitrary"),
        name="sgu_in",
    )(x2d, g.reshape(1, d), w_bf16)


def _sgu_out_kernel(x_ref, u_ref, v_ref, lng_ref, lnb_ref, ws_ref, bs_ref, w_ref, o_ref, gated_scr):
    tm = x_ref.shape[0]
    gd = SGU_GROUP_DIM
    v = v_ref[...].astype(F32)
    mu = jnp.mean(v, axis=-1, keepdims=True)
    vc = v - mu
    vn = vc * lax.rsqrt(jnp.mean(vc * vc, axis=-1, keepdims=True) + EPS) * lng_ref[...] + lnb_ref[...]
    vn = vn.astype(BF16)

    t_chunk = lax.broadcasted_iota(jnp.int32, (SGU_BLOCK, SGU_BLOCK), 0) // CHUNK
    s_chunk = lax.broadcasted_iota(jnp.int32, (SGU_BLOCK, SGU_BLOCK), 1) // CHUNK
    allowed = t_chunk >= s_chunk
    n_blk = tm // SGU_BLOCK
    for grp in range(SGU_GROUPS):
        cols = slice(grp * gd, (grp + 1) * gd)
        ws = jnp.where(allowed, ws_ref[grp], 0.0).astype(BF16)
        vg = jnp.concatenate([vn[blk * SGU_BLOCK:(blk + 1) * SGU_BLOCK, cols] for blk in range(n_blk)],
                             axis=1)
        mixed = _dot(ws, vg)
        for blk in range(n_blk):
            rows = slice(blk * SGU_BLOCK, (blk + 1) * SGU_BLOCK)
            sv = mixed[:, blk * gd:(blk + 1) * gd] + bs_ref[:, cols]
            gated_scr[rows, cols] = (u_ref[rows, cols].astype(F32) * sv).astype(BF16)
    o_ref[...] = x_ref[...] + _dot(gated_scr[...], w_ref[...])


def _sgu_out(x2d, uv, ln_g, ln_b, w_s, b_s, w_bf16, tm=512):
    t, d = x2d.shape
    bs_exp = jnp.repeat(b_s.T.astype(F32), SGU_GROUP_DIM, axis=1)
    return pl.pallas_call(
        _sgu_out_kernel,
        grid=(t // tm,),
        in_specs=[pl.BlockSpec((tm, d), lambda i: (i, 0)),
                  pl.BlockSpec((tm, d), lambda i: (i, 0)),
                  pl.BlockSpec((tm, d), lambda i: (i, 1)),
                  _const_spec((1, d)), _const_spec((1, d)),
                  _const_spec(w_s.shape), _const_spec(bs_exp.shape),
                  _const_spec((d, d))],
        out_specs=pl.BlockSpec((tm, d), lambda i: (i, 0)),
        out_shape=jax.ShapeDtypeStruct((t, d), F32),
        scratch_shapes=[pltpu.VMEM((tm, d), BF16)],
        compiler_params=_params("parallel"),
        name="sgu_out",
    )(x2d, uv, uv, ln_g.reshape(1, d), ln_b.reshape(1, d), w_s, bs_exp, w_bf16)


def _top2_gates(logits):
    lane = lax.broadcasted_iota(jnp.int32, logits.shape, 1)
    m1 = jnp.max(logits, axis=-1, keepdims=True)
    i1 = jnp.min(jnp.where(logits == m1, lane, LANES), axis=-1, keepdims=True)
    rest = jnp.where(lane == i1, -jnp.inf, logits)
    m2 = jnp.max(rest, axis=-1, keepdims=True)
    i2 = jnp.min(jnp.where(rest == m2, lane, LANES), axis=-1, keepdims=True)
    e2 = jnp.exp(m2 - m1)
    denom = 1.0 + e2
    return jnp.where(lane == i1, 1.0 / denom, 0.0) + jnp.where(lane == i2, e2 / denom, 0.0)


def _moe_dense_kernel(x_ref, g_ref, wr_ref, wg_ref, wu_ref, wd_ref, fg_ref, o_ref,
                      h_scr, gates_scr, ge_scr):
    e = pl.program_id(1)
    j = pl.program_id(2)
    tm = x_ref.shape[0]

    @pl.when((e == 0) & (j == 0))
    def _():
        x = x_ref[...]
        h = _rms(x, g_ref[...])
        h_scr[...] = h.astype(BF16)
        o_ref[...] = x
        logits = jnp.dot(h, wr_ref[...], preferred_element_type=F32, precision=lax.Precision.HIGHEST)
        lane = lax.broadcasted_iota(jnp.int32, logits.shape, 1)
        gates_scr[...] = _top2_gates(jnp.where(lane < N_EXPERTS, logits, -jnp.inf))

    @pl.when(j == 0)
    def _():
        lane = lax.broadcasted_iota(jnp.int32, (tm, LANES), 1)
        ge = jnp.sum(jnp.where(lane == e, gates_scr[...], 0.0), axis=-1, keepdims=True)
        ge_scr[...] = jnp.broadcast_to(ge, (tm, LANES))

    h = h_scr[...]
    a = _dot(h, wg_ref[...].astype(BF16))
    u = _dot(h, wu_ref[...].astype(BF16))
    tf = a.shape[1]
    ge = jnp.concatenate([ge_scr[...]] * (tf // LANES), axis=1)
    o_ref[...] += _dot((_silu(a) * u * ge).astype(BF16), wd_ref[...].astype(BF16))

    @pl.when((e == pl.num_programs(1) - 1) & (j == pl.num_programs(2) - 1))
    def _():
        o_ref[...] = _rms(o_ref[...], fg_ref[...])


def _moe_dense(x2d, g, w_router, w_gate, w_up, w_down, final_g, tm=512, tf=512):
    t, d = x2d.shape
    ne, _, dff = w_gate.shape
    wr_pad = jnp.zeros((d, LANES), F32).at[:, :ne].set(w_router.astype(F32))
    return pl.pallas_call(
        _moe_dense_kernel,
        grid=(t // tm, ne, dff // tf),
        in_specs=[pl.BlockSpec((tm, d), lambda i, e, j: (i, 0)),
                  _const_spec((1, d)),
                  _const_spec((d, LANES)),
                  pl.BlockSpec((None, d, tf), lambda i, e, j: (e, 0, j)),
                  pl.BlockSpec((None, d, tf), lambda i, e, j: (e, 0, j)),
                  pl.BlockSpec((None, tf, d), lambda i, e, j: (e, j, 0)),
                  _const_spec((1, d))],
        out_specs=pl.BlockSpec((tm, d), lambda i, e, j: (i, 0)),
        out_shape=jax.ShapeDtypeStruct((t, d), F32),
        scratch_shapes=[pltpu.VMEM((tm, d), BF16),
                        pltpu.VMEM((tm, LANES), F32),
                        pltpu.VMEM((tm, LANES), F32)],
        compiler_params=_params("parallel", "arbitrary", "arbitrary"),
        name="moe_dense",
    )(x2d, g.reshape(1, d), wr_pad, w_gate, w_up, w_down, final_g.reshape(1, d))


def kernel(x, lb_logits, l0_norm_mix, l0_hgrn_w_in, l0_hgrn_norm, l0_hgrn_w_out, l0_norm_ffn, l0_ffn_w_gate, l0_ffn_w_up, l0_ffn_w_down, l1_norm_mix, l1_sgu_w_in, l1_sgu_ln_g, l1_sgu_ln_b, l1_sgu_w_s, l1_sgu_b_s, l1_sgu_w_out, l1_norm_ffn, l1_moe_w_router, l1_moe_w_gate, l1_moe_w_up, l1_moe_w_down, final_norm):
    bsz, seq, d = x.shape
    x0 = x.reshape(bsz * seq, d)

    h0 = _rms_cast(x0, l0_norm_mix).reshape(bsz, seq, d)
    og = _hgrn(h0, l0_hgrn_w_in, lb_logits.astype(F32), l0_hgrn_norm)
    x1 = _proj_res(x0, og.reshape(bsz * seq, d), l0_hgrn_w_out.astype(BF16))
    x2 = _ffn(x1, l0_norm_ffn, l0_ffn_w_gate.astype(BF16), l0_ffn_w_up.astype(BF16),
              l0_ffn_w_down.astype(BF16))

    uv = _sgu_in(x2, l1_norm_mix, l1_sgu_w_in.astype(BF16))
    x3 = _sgu_out(x2, uv, l1_sgu_ln_g, l1_sgu_ln_b, l1_sgu_w_s, l1_sgu_b_s, l1_sgu_w_out.astype(BF16))
    out = _moe_dense(x3, l1_norm_ffn, l1_moe_w_router, l1_moe_w_gate, l1_moe_w_up, l1_moe_w_down,
                     final_norm)
    return out.reshape(bsz, seq, d)
```

```python
import functools
import math

import jax
import jax.numpy as jnp
from jax import lax
from jax.experimental import pallas as pl
from jax.experimental.pallas import tpu as pltpu

D_MODEL = 2048
CHUNK = 64
EPS = 1e-6
HGRN_HEADS = 16
HEAD_DIM = 128
SGU_BLOCK = 128
SGU_GROUPS = 16
SGU_GROUP_DIM = D_MODEL // SGU_GROUPS
N_EXPERTS = 8
LANES = 128
V7X_VMEM_LIMIT_BYTES = 56 * 1024 * 1024

BF16 = jnp.bfloat16
F32 = jnp.float32


def _params(*sem):
    return pltpu.CompilerParams(dimension_semantics=sem, vmem_limit_bytes=V7X_VMEM_LIMIT_BYTES)


def _const_spec(shape):
    nd = len(shape)
    return pl.BlockSpec(shape, lambda *_: (0,) * nd, pipeline_mode=pl.Buffered(1))


def _rms(x, g):
    return x * lax.rsqrt(jnp.mean(x * x, axis=-1, keepdims=True) + EPS) * g


def _sigmoid_pair(f):
    z = jnp.exp(-jnp.abs(f))
    r = 1.0 / (1.0 + z)
    zr = z * r
    pos = f >= 0
    return jnp.where(pos, r, zr), jnp.where(pos, zr, r)


def _silu(x):
    return x * _sigmoid_pair(x)[0]


def _dot(a, b):
    return jnp.dot(a, b, preferred_element_type=F32)


def _rms_cast_kernel(x_ref, g_ref, o_ref):
    o_ref[...] = _rms(x_ref[...], g_ref[...]).astype(o_ref.dtype)


def _rms_cast(x2d, g, tm=512):
    t, d = x2d.shape
    return pl.pallas_call(
        _rms_cast_kernel,
        grid=(t // tm,),
        in_specs=[pl.BlockSpec((tm, d), lambda i: (i, 0)), _const_spec((1, d))],
        out_specs=pl.BlockSpec((tm, d), lambda i: (i, 0)),
        out_shape=jax.ShapeDtypeStruct((t, d), BF16),
        compiler_params=_params("parallel"),
        name="rms_cast",
    )(x2d, g.reshape(1, d))


def _chunk_cumsum(g):
    row = lax.broadcasted_iota(jnp.int32, g.shape, 0) & (CHUNK - 1)
    b = g
    s = 1
    while s < CHUNK:
        b = b + jnp.where(row >= s, pltpu.roll(b, s, axis=0), 0.0)
        s *= 2
    return b


def _hgrn_kernel(h_ref, wq_ref, wf_ref, wi_ref, wg_ref, lbl_ref, ng_ref, o_ref,
                 w_scr, proj_scr, s_scr, *, group):
    seq = h_ref.shape[1]
    dk = HEAD_DIM
    for n, w_ref in enumerate((wq_ref, wf_ref, wi_ref, wg_ref)):
        w_scr[:, n * dk:(n + 1) * dk] = w_ref[...].astype(BF16)
    proj_scr[...] = _dot(h_ref[0], w_scr[...])

    lbl = lbl_ref[...]
    lbe = jnp.exp(lbl - jnp.max(lbl, axis=0, keepdims=True))
    lb = lbe[0:1, :] / jnp.sum(lbe, axis=0, keepdims=True)
    ng = ng_ref[...]

    rows_per_group = group * CHUNK
    t_idx = lax.broadcasted_iota(jnp.int32, (group, CHUNK, CHUNK), 1)
    s_idx = lax.broadcasted_iota(jnp.int32, (group, CHUNK, CHUNK), 2)
    causal = t_idx >= s_idx
    st = jnp.zeros((dk, dk), F32)
    for grp in range(seq // rows_per_group):
        rows = pl.ds(grp * rows_per_group, rows_per_group)
        q = _silu(proj_scr[rows, 0 * dk:1 * dk])
        sig, nsig = _sigmoid_pair(proj_scr[rows, 1 * dk:2 * dk])
        v = proj_scr[rows, 2 * dk:3 * dk]
        gate = proj_scr[rows, 3 * dk:4 * dk]
        g = jnp.log(lb + (1.0 - lb) * sig)
        k = (1.0 - lb) * nsig
        b = _chunk_cumsum(g).reshape(group, CHUNK, dk)
        b_mid = b[:, CHUNK // 2 - 1:CHUNK // 2, :]
        b_last = b[:, CHUNK - 1:CHUNK, :]
        q_intra = q.reshape(group, CHUNK, dk) * jnp.exp(b - b_mid)
        k_intra = k.reshape(group, CHUNK, dk) * jnp.exp(b_mid - b)
        q_inter = (q_intra * jnp.exp(b_mid)).astype(BF16)
        k_state = (k_intra * jnp.exp(b_last - b_mid)).astype(BF16)
        decay = jnp.exp(b_last)
        vb = v.reshape(group, CHUNK, dk).astype(BF16)

        scores = jnp.einsum("gtk,gsk->gts", q_intra.astype(BF16), k_intra.astype(BF16),
                            preferred_element_type=F32)
        scores = jnp.where(causal, scores, 0.0).astype(BF16)
        o = jnp.einsum("gts,gsv->gtv", scores, vb, preferred_element_type=F32)
        d_st = jnp.einsum("gsv,gsk->gvk", vb, k_state, preferred_element_type=F32)
        for c in range(group):
            s_scr[c] = st.astype(BF16)
            st = st * decay[c] + d_st[c]
        o = o + jnp.einsum("gtk,gvk->gtv", q_inter, s_scr[...], preferred_element_type=F32)

        o = o.reshape(rows_per_group, dk)
        o = o * lax.rsqrt(jnp.mean(o * o, axis=-1, keepdims=True) + EPS)
        o_ref[0, rows, :] = (o * ng * _silu(gate)).astype(o_ref.dtype)


def _hgrn(h, w_in, lb_logits, norm_g, group=8):
    bsz, seq, d = h.shape
    dk = HEAD_DIM
    nh = HGRN_HEADS
    w_specs = [pl.BlockSpec((d, dk), lambda b, hd, n=n: (0, n * nh + hd)) for n in range(4)]
    return pl.pallas_call(
        functools.partial(_hgrn_kernel, group=group),
        grid=(bsz, nh),
        in_specs=[pl.BlockSpec((1, seq, d), lambda b, hd: (b, 0, 0))] + w_specs + [
            pl.BlockSpec((lb_logits.shape[0], dk), lambda b, hd: (0, hd)),
            pl.BlockSpec((1, dk), lambda b, hd: (0, hd)),
        ],
        out_specs=pl.BlockSpec((1, seq, dk), lambda b, hd: (b, 0, hd)),
        out_shape=jax.ShapeDtypeStruct((bsz, seq, nh * dk), BF16),
        scratch_shapes=[
            pltpu.VMEM((d, 4 * dk), BF16),
            pltpu.VMEM((seq, 4 * dk), F32),
            pltpu.VMEM((group, dk, dk), BF16),
        ],
        compiler_params=_params("parallel", "arbitrary"),
        name="hgrn2",
    )(h, w_in, w_in, w_in, w_in, lb_logits, norm_g.reshape(1, -1))


def _proj_res_kernel(x_ref, a_ref, w_ref, o_ref):
    o_ref[...] = x_ref[...] + _dot(a_ref[...], w_ref[...])


def _proj_res(x2d, a2d, w_bf16, tm=512):
    t, d = x2d.shape
    k = a2d.shape[1]
    return pl.pallas_call(
        _proj_res_kernel,
        grid=(t // tm,),
        in_specs=[pl.BlockSpec((tm, d), lambda i: (i, 0)),
                  pl.BlockSpec((tm, k), lambda i: (i, 0)),
                  _const_spec((k, d))],
        out_specs=pl.BlockSpec((tm, d), lambda i: (i, 0)),
        out_shape=jax.ShapeDtypeStruct((t, d), F32),
        compiler_params=_params("parallel"),
        name="proj_res",
    )(x2d, a2d, w_bf16)


def _ffn_kernel(x_ref, g_ref, wg_ref, wu_ref, wd_ref, o_ref, h_scr):
    @pl.when(pl.program_id(1) == 0)
    def _():
        x = x_ref[...]
        h_scr[...] = _rms(x, g_ref[...]).astype(BF16)
        o_ref[...] = x

    h = h_scr[...]
    a = _dot(h, wg_ref[...])
    u = _dot(h, wu_ref[...])
    o_ref[...] += _dot((_silu(a) * u).astype(BF16), wd_ref[...])


def _ffn(x2d, g, w_gate, w_up, w_down, tm=512, tf=512):
    t, d = x2d.shape
    dff = w_gate.shape[1]
    return pl.pallas_call(
        _ffn_kernel,
        grid=(t // tm, dff // tf),
        in_specs=[pl.BlockSpec((tm, d), lambda i, j: (i, 0)),
                  _const_spec((1, d)),
                  pl.BlockSpec((d, tf), lambda i, j: (0, j)),
                  pl.BlockSpec((d, tf), lambda i, j: (0, j)),
                  pl.BlockSpec((tf, d), lambda i, j: (j, 0))],
        out_specs=pl.BlockSpec((tm, d), lambda i, j: (i, 0)),
        out_shape=jax.ShapeDtypeStruct((t, d), F32),
        scratch_shapes=[pltpu.VMEM((tm, d), BF16)],
        compiler_params=_params("parallel", "arbitrary"),
        name="ffn_dense",
    )(x2d, g.reshape(1, d), w_gate, w_up, w_down)


def _sgu_in_kernel(x_ref, g_ref, w_ref, o_ref, h_scr):
    @pl.when(pl.program_id(1) == 0)
    def _():
        h_scr[...] = _rms(x_ref[...], g_ref[...]).astype(BF16)

    y = _dot(h_scr[...], w_ref[...])
    o_ref[...] = (0.5 * y * (1.0 + lax.erf(y * math.sqrt(0.5)))).astype(o_ref.dtype)


def _sgu_in(x2d, g, w_bf16, tm=512, tn=1024):
    t, d = x2d.shape
    n = w_bf16.shape[1]
    return pl.pallas_call(
        _sgu_in_kernel,
        grid=(t // tm, n // tn),
        in_specs=[pl.BlockSpec((tm, d), lambda i, j: (i, 0)),
                  _const_spec((1, d)),
                  pl.BlockSpec((d, tn), lambda i, j: (0, j))],
        out_specs=pl.BlockSpec((tm, tn), lambda i, j: (i, j)),
        out_shape=jax.ShapeDtypeStruct((t, n), BF16),
        scratch_shapes=[pltpu.VMEM((tm, d), BF16)],
        compiler_params=_params("parallel", "arbitrary"),
        name="sgu_in",
    )(x2d, g.reshape(1, d), w_bf16)


def _sgu_out_kernel(x_ref, u_ref, v_ref, lng_ref, lnb_ref, ws_ref, bs_ref, w_ref, o_ref, gated_scr):
    tm = x_ref.shape[0]
    gd = SGU_GROUP_DIM
    v = v_ref[...].astype(F32)
    mu = jnp.mean(v, axis=-1, keepdims=True)
    vc = v - mu
    vn = vc * lax.rsqrt(jnp.mean(vc * vc, axis=-1, keepdims=True) + EPS) * lng_ref[...] + lnb_ref[...]
    vn = vn.astype(BF16)

    t_chunk = lax.broadcasted_iota(jnp.int32, (SGU_BLOCK, SGU_BLOCK), 0) // CHUNK
    s_chunk = lax.broadcasted_iota(jnp.int32, (SGU_BLOCK, SGU_BLOCK), 1) // CHUNK
    allowed = t_chunk >= s_chunk
    n_blk = tm // SGU_BLOCK
    for grp in range(SGU_GROUPS):
        cols = slice(grp * gd, (grp + 1) * gd)
        ws = jnp.where(allowed, ws_ref[grp], 0.0).astype(BF16)
        vg = jnp.concatenate([vn[blk * SGU_BLOCK:(blk + 1) * SGU_BLOCK, cols] for blk in range(n_blk)],
                             axis=1)
        mixed = _dot(ws, vg)
        for blk in range(n_blk):
            rows = slice(blk * SGU_BLOCK, (blk + 1) * SGU_BLOCK)
            sv = mixed[:, blk * gd:(blk + 1) * gd] + bs_ref[:, cols]
            gated_scr[rows, cols] = (u_ref[rows, cols].astype(F32) * sv).astype(BF16)
    o_ref[...] = x_ref[...] + _dot(gated_scr[...], w_ref[...])


def _sgu_out(x2d, uv, ln_g, ln_b, w_s, b_s, w_bf16, tm=512):
    t, d = x2d.shape
    bs_exp = jnp.repeat(b_s.T.astype(F32), SGU_GROUP_DIM, axis=1)
    return pl.pallas_call(
        _sgu_out_kernel,
        grid=(t // tm,),
        in_specs=[pl.BlockSpec((tm, d), lambda i: (i, 0)),
                  pl.BlockSpec((tm, d), lambda i: (i, 0)),
                  pl.BlockSpec((tm, d), lambda i: (i, 1)),
                  _const_spec((1, d)), _const_spec((1, d)),
                  _const_spec(w_s.shape), _const_spec(bs_exp.shape),
                  _const_spec((d, d))],
        out_specs=pl.BlockSpec((tm, d), lambda i: (i, 0)),
        out_shape=jax.ShapeDtypeStruct((t, d), F32),
        scratch_shapes=[pltpu.VMEM((tm, d), BF16)],
        compiler_params=_params("parallel"),
        name="sgu_out",
    )(x2d, uv, uv, ln_g.reshape(1, d), ln_b.reshape(1, d), w_s, bs_exp, w_bf16)


def _pack_bf16_pairs(h):
    c = h.shape[1] // 2
    bits = pltpu.bitcast(h.astype(BF16).astype(F32), jnp.uint32)
    return bits[:, :c] | (bits[:, c:] >> 16)


def _unpack_bf16_pairs(w):
    hi = pltpu.bitcast(w & jnp.uint32(0xFFFF0000), F32)
    lo = pltpu.bitcast(w << 16, F32)
    return hi.astype(BF16), lo.astype(BF16)


def _router_kernel(x_ref, g_ref, wr_ref, hp_ref, idx_ref, wgt_ref):
    h = _rms(x_ref[...], g_ref[...])
    hp_ref[...] = _pack_bf16_pairs(h)
    logits = jnp.dot(h, wr_ref[...], preferred_element_type=F32, precision=lax.Precision.HIGHEST)
    lane = lax.broadcasted_iota(jnp.int32, logits.shape, 1)
    logits = jnp.where(lane < N_EXPERTS, logits, -jnp.inf)
    m1 = jnp.max(logits, axis=-1, keepdims=True)
    i1 = jnp.min(jnp.where(logits == m1, lane, LANES), axis=-1, keepdims=True)
    rest = jnp.where(lane == i1, -jnp.inf, logits)
    m2 = jnp.max(rest, axis=-1, keepdims=True)
    i2 = jnp.min(jnp.where(rest == m2, lane, LANES), axis=-1, keepdims=True)
    e2 = jnp.exp(m2 - m1)
    denom = 1.0 + e2
    idx_ref[...] = jnp.where(lane == 0, i1, jnp.where(lane == 1, i2, 0))
    wgt_ref[...] = jnp.where(lane == 0, 1.0 / denom, jnp.where(lane == 1, e2 / denom, 0.0))


def _router(x2d, g, w_router, tm=512):
    t, d = x2d.shape
    wr_pad = jnp.zeros((d, LANES), F32).at[:, :w_router.shape[1]].set(w_router.astype(F32))
    return pl.pallas_call(
        _router_kernel,
        grid=(t // tm,),
        in_specs=[pl.BlockSpec((tm, d), lambda i: (i, 0)), _const_spec((1, d)), _const_spec((d, LANES))],
        out_specs=[pl.BlockSpec((tm, d // 2), lambda i: (i, 0)),
                   pl.BlockSpec((tm, LANES), lambda i: (i, 0)),
                   pl.BlockSpec((tm, LANES), lambda i: (i, 0))],
        out_shape=[jax.ShapeDtypeStruct((t, d // 2), jnp.uint32),
                   jax.ShapeDtypeStruct((t, LANES), jnp.int32),
                   jax.ShapeDtypeStruct((t, LANES), F32)],
        compiler_params=_params("parallel"),
        name="moe_router",
    )(x2d, g.reshape(1, d), wr_pad)


def _routing_plan(idx, tm):
    t = idx.shape[0]
    e_flat = jnp.concatenate([idx[:, 0], idx[:, 1]])
    onehot = (e_flat[:, None] == jnp.arange(N_EXPERTS, dtype=jnp.int32)[None, :]).astype(jnp.int32)
    csum = jnp.cumsum(onehot, axis=0)
    count = csum[-1]
    rank = jnp.sum(onehot * csum, axis=1) - 1
    n_tile = (count + tm - 1) // tm
    tile_end = jnp.cumsum(n_tile)
    row_start = (tile_end - n_tile) * tm
    slot = jnp.sum(onehot * row_start[None, :], axis=1) + rank
    n_tiles_max = (2 * t) // tm + N_EXPERTS
    n_used = tile_end[-1]
    tile_id = jnp.minimum(jnp.arange(n_tiles_max, dtype=jnp.int32), n_used - 1)
    tile_expert = jnp.sum((tile_id[:, None] >= tile_end[None, :]).astype(jnp.int32), axis=1)
    return slot.astype(jnp.int32), tile_expert.astype(jnp.int32), n_used.reshape(1).astype(jnp.int32)


def _row_copy(src_ref, src_row, dst_ref, dst_row, sem):
    return pltpu.make_async_copy(src_ref.at[pl.ds(src_row, 1)], dst_ref.at[pl.ds(dst_row, 1)], sem)


def _dispatch_kernel(slot_ref, hp_ref, xs_in_ref, xs_ref, sem):
    del xs_in_ref
    td = hp_ref.shape[0]

    def start(r, carry):
        _row_copy(hp_ref, r, xs_ref, slot_ref[0, 0, r], sem.at[0]).start()
        _row_copy(hp_ref, r, xs_ref, slot_ref[0, 1, r], sem.at[0]).start()
        return carry

    def wait(r, carry):
        _row_copy(hp_ref, r, xs_ref, slot_ref[0, 0, r], sem.at[0]).wait()
        _row_copy(hp_ref, r, xs_ref, slot_ref[0, 1, r], sem.at[0]).wait()
        return carry

    lax.fori_loop(0, td, start, 0)
    lax.fori_loop(0, td, wait, 0)


def _dispatch(hp, slot, n_rows, td=512):
    t, c = hp.shape
    slots = jnp.stack([slot[:t].reshape(t // td, td), slot[t:].reshape(t // td, td)], axis=1)
    return pl.pallas_call(
        _dispatch_kernel,
        grid=(t // td,),
        in_specs=[pl.BlockSpec((1, 2, td), lambda i: (i, 0, 0), memory_space=pltpu.SMEM),
                  pl.BlockSpec((td, c), lambda i: (i, 0)),
                  pl.BlockSpec(memory_space=pl.ANY)],
        out_specs=pl.BlockSpec(memory_space=pl.ANY),
        out_shape=jax.ShapeDtypeStruct((n_rows, c), jnp.uint32),
        scratch_shapes=[pltpu.SemaphoreType.DMA((1,))],
        input_output_aliases={2: 0},
        compiler_params=_params("arbitrary"),
        name="moe_dispatch",
    )(slots, hp, jnp.zeros((n_rows, c), jnp.uint32))


def _moe_ffn_kernel(expert_ref, nused_ref, xs_ref, wg_ref, wu_ref, wd_ref, y_ref, h_scr):
    del expert_ref
    c = xs_ref.shape[1]

    @pl.when((pl.program_id(0) >= nused_ref[0]) & (pl.program_id(1) == 0))
    def _():
        y_ref[...] = jnp.zeros_like(y_ref)

    @pl.when(pl.program_id(0) < nused_ref[0])
    def _():
        @pl.when(pl.program_id(1) == 0)
        def _():
            hi, lo = _unpack_bf16_pairs(xs_ref[...])
            h_scr[:, :c] = hi
            h_scr[:, c:] = lo
            y_ref[...] = jnp.zeros_like(y_ref)

        h = h_scr[...]
        a = _dot(h, wg_ref[...].astype(BF16))
        u = _dot(h, wu_ref[...].astype(BF16))
        y_ref[...] += _dot((_silu(a) * u).astype(BF16), wd_ref[...].astype(BF16))


def _moe_ffn(xs, tile_expert, n_used, w_gate, w_up, w_down, tm, tf=256):
    n_rows, c = xs.shape
    d = 2 * c
    dff = w_gate.shape[2]
    nj = dff // tf

    def w_col(i, j, expert, nused):
        return (expert[i], 0, jnp.where(i < nused[0], j, nj - 1))

    def w_row(i, j, expert, nused):
        return (expert[i], jnp.where(i < nused[0], j, nj - 1), 0)

    return pl.pallas_call(
        _moe_ffn_kernel,
        grid_spec=pltpu.PrefetchScalarGridSpec(
            num_scalar_prefetch=2,
            grid=(n_rows // tm, nj),
            in_specs=[pl.BlockSpec((tm, c), lambda i, j, expert, nused: (i, 0)),
                      pl.BlockSpec((None, d, tf), w_col),
                      pl.BlockSpec((None, d, tf), w_col),
                      pl.BlockSpec((None, tf, d), w_row)],
            out_specs=pl.BlockSpec((tm, d), lambda i, j, expert, nused: (i, 0)),
            scratch_shapes=[pltpu.VMEM((tm, d), BF16)],
        ),
        out_shape=jax.ShapeDtypeStruct((n_rows, d), F32),
        compiler_params=_params("arbitrary", "arbitrary"),
        name="moe_ffn",
    )(tile_expert, n_used, xs, w_gate, w_up, w_down)


def _combine_kernel(slot_ref, x_ref, wgt_ref, fg_ref, y_ref, o_ref, y1_scr, y2_scr, sem):
    tc = x_ref.shape[0]

    def start(r, carry):
        _row_copy(y_ref, slot_ref[0, 0, r], y1_scr, r, sem.at[0]).start()
        _row_copy(y_ref, slot_ref[0, 1, r], y2_scr, r, sem.at[0]).start()
        return carry

    def wait(r, carry):
        _row_copy(y_ref, slot_ref[0, 0, r], y1_scr, r, sem.at[0]).wait()
        _row_copy(y_ref, slot_ref[0, 1, r], y2_scr, r, sem.at[0]).wait()
        return carry

    lax.fori_loop(0, tc, start, 0)
    lax.fori_loop(0, tc, wait, 0)
    wgt = wgt_ref[...]
    x = x_ref[...] + wgt[:, 0:1] * y1_scr[...] + wgt[:, 1:2] * y2_scr[...]
    o_ref[...] = _rms(x, fg_ref[...])


def _combine(x2d, wgt, slot, y, final_g, tc=256):
    t, d = x2d.shape
    slots = jnp.stack([slot[:t].reshape(t // tc, tc), slot[t:].reshape(t // tc, tc)], axis=1)
    return pl.pallas_call(
        _combine_kernel,
        grid=(t // tc,),
        in_specs=[pl.BlockSpec((1, 2, tc), lambda i: (i, 0, 0), memory_space=pltpu.SMEM),
                  pl.BlockSpec((tc, d), lambda i: (i, 0)),
                  pl.BlockSpec((tc, LANES), lambda i: (i, 0)),
                  _const_spec((1, d)),
                  pl.BlockSpec(memory_space=pl.ANY)],
        out_specs=pl.BlockSpec((tc, d), lambda i: (i, 0)),
        out_shape=jax.ShapeDtypeStruct((t, d), F32),
        scratch_shapes=[pltpu.VMEM((tc, d), F32), pltpu.VMEM((tc, d), F32), pltpu.SemaphoreType.DMA((1,))],
        compiler_params=_params("arbitrary"),
        name="moe_combine",
    )(slots, x2d, wgt, final_g.reshape(1, d), y)


def _moe_routed(x2d, g, w_router, w_gate, w_up, w_down, final_g, tm=1024):
    t = x2d.shape[0]
    hp, idx, wgt = _router(x2d, g, w_router)
    slot, tile_expert, n_used = _routing_plan(idx, tm)
    n_rows = (2 * t // tm + N_EXPERTS) * tm
    xs = _dispatch(hp, slot, n_rows)
    y = _moe_ffn(xs, tile_expert, n_used, w_gate, w_up, w_down, tm)
    return _combine(x2d, wgt, slot, y, final_g)


def kernel(x, lb_logits, l0_norm_mix, l0_hgrn_w_in, l0_hgrn_norm, l0_hgrn_w_out, l0_norm_ffn, l0_ffn_w_gate, l0_ffn_w_up, l0_ffn_w_down, l1_norm_mix, l1_sgu_w_in, l1_sgu_ln_g, l1_sgu_ln_b, l1_sgu_w_s, l1_sgu_b_s, l1_sgu_w_out, l1_norm_ffn, l1_moe_w_router, l1_moe_w_gate, l1_moe_w_up, l1_moe_w_down, final_norm):
    bsz, seq, d = x.shape
    x0 = x.reshape(bsz * seq, d)

    h0 = _rms_cast(x0, l0_norm_mix).reshape(bsz, seq, d)
    og = _hgrn(h0, l0_hgrn_w_in, lb_logits.astype(F32), l0_hgrn_norm)
    x1 = _proj_res(x0, og.reshape(bsz * seq, d), l0_hgrn_w_out.astype(BF16))
    x2 = _ffn(x1, l0_norm_ffn, l0_ffn_w_gate.astype(BF16), l0_ffn_w_up.astype(BF16),
              l0_ffn_w_down.astype(BF16))

    uv = _sgu_in(x2, l1_norm_mix, l1_sgu_w_in.astype(BF16))
    x3 = _sgu_out(x2, uv, l1_sgu_ln_g, l1_sgu_ln_b, l1_sgu_w_s, l1_sgu_b_s, l1_sgu_w_out.astype(BF16))
    out = _moe_routed(x3, l1_norm_ffn, l1_moe_w_router, l1_moe_w_gate, l1_moe_w_up, l1_moe_w_down,
                      final_norm)
    return out.reshape(bsz, seq, d)
```
